```python
import jax, jax.numpy as jnp
from jax import lax
import numpy as np

D_MODEL = 4096
BATCH = 1
SEQ = 8192
DEPTH = 2

HEAD_DIM = 128
N_HEADS_FOX = 12
N_HEADS_SB = 10
N_HEADS_MLA = 10
D_FOX = N_HEADS_FOX * HEAD_DIM
D_SB = N_HEADS_SB * HEAD_DIM
Q_LORA = 1536
KV_LORA = 512
QK_NOPE = 128
QK_ROPE = 64
QK_MLA = QK_NOPE + QK_ROPE
V_MLA = HEAD_DIM
D_MLA = N_HEADS_MLA * V_MLA
D_MIX = D_FOX + D_SB + D_MLA
N_HEADS_TOTAL = N_HEADS_FOX + N_HEADS_SB + N_HEADS_MLA
ROPE_THETA = 10000.0
BLOCK_Q = 128
EPS = 1e-6
IN_SIZES = [D_FOX, D_FOX, D_FOX, N_HEADS_FOX, D_SB, D_SB, D_SB, Q_LORA, KV_LORA, QK_ROPE, D_MIX]
N_IN = int(sum(IN_SIZES))
IN_SPLITS = [int(v) for v in np.cumsum(IN_SIZES)[:-1]]

kernel_name = "hybrid_fox_stickbreak_mla_block"


def rms_norm(x, g):
    xf = x.astype(jnp.float32)
    y = xf * lax.rsqrt(jnp.mean(xf * xf, axis=-1, keepdims=True) + EPS)
    return (y * g.astype(jnp.float32)).astype(x.dtype)


def rope(x, positions):
    half = x.shape[-1] // 2
    inv = ROPE_THETA ** (-jnp.arange(half, dtype=jnp.float32) / half)
    ang = positions.astype(jnp.float32)[..., None] * inv
    cos = jnp.cos(ang)[:, :, None, :]
    sin = jnp.sin(ang)[:, :, None, :]
    xf = x.astype(jnp.float32)
    x1, x2 = xf[..., :half], xf[..., half:]
    return jnp.concatenate([x1 * cos - x2 * sin, x2 * cos + x1 * sin], axis=-1).astype(x.dtype)


def to_heads(t, n_heads, d):
    return t.reshape(t.shape[0], t.shape[1], n_heads, d)


def block_sweep(block_fn, seq):
    out = lax.map(block_fn, jnp.arange(seq // BLOCK_Q))
    nb, b, h, bq, dv = out.shape
    return out.transpose(1, 0, 3, 2, 4).reshape(b, nb * bq, h, dv)


def forgetting_attention(q, k, v, log_f):
    seq = q.shape[2]
    scale = q.shape[-1] ** -0.5
    cum = jnp.cumsum(log_f, axis=-1)
    kpos = jnp.arange(seq)

    def block(i):
        start = i * BLOCK_Q
        qb = lax.dynamic_slice_in_dim(q, start, BLOCK_Q, axis=2)
        cb = lax.dynamic_slice_in_dim(cum, start, BLOCK_Q, axis=2)
        qpos = start + jnp.arange(BLOCK_Q)
        s = jnp.einsum('bhqd,bhkd->bhqk', qb, k, preferred_element_type=jnp.float32) * scale
        s = s + (cb[..., :, None] - cum[..., None, :])
        s = jnp.where(kpos[None, :] <= qpos[:, None], s, -jnp.inf)
        p = jax.nn.softmax(s, axis=-1)
        return jnp.einsum('bhqk,bhkd->bhqd', p.astype(v.dtype), v)

    return block_sweep(block, seq)


def stick_breaking_attention(q, k, v):
    seq = q.shape[2]
    scale = q.shape[-1] ** -0.5
    kpos = jnp.arange(seq)

    def block(i):
        start = i * BLOCK_Q
        qb = lax.dynamic_slice_in_dim(q, start, BLOCK_Q, axis=2)
        qpos = start + jnp.arange(BLOCK_Q)
        causal = kpos[None, :] < qpos[:, None]
        z = jnp.einsum('bhqd,bhkd->bhqk', qb, k, preferred_element_type=jnp.float32) * scale
        log_1m_beta = jnp.where(causal, jax.nn.log_sigmoid(-z), 0.0)
        tail = lax.cumsum(log_1m_beta, axis=3, reverse=True) - log_1m_beta
        log_a = jnp.where(causal, jax.nn.log_sigmoid(z) + tail, -jnp.inf)
        a = jnp.exp(log_a)
        return jnp.einsum('bhqk,bhkd->bhqd', a.astype(v.dtype), v)

    return block_sweep(block, seq)


def causal_softmax_attention(q, k, v):
    seq = q.shape[2]
    scale = q.shape[-1] ** -0.5
    kpos = jnp.arange(seq)

    def block(i):
        start = i * BLOCK_Q
        qb = lax.dynamic_slice_in_dim(q, start, BLOCK_Q, axis=2)
        qpos = start + jnp.arange(BLOCK_Q)
        s = jnp.einsum('bhqd,bhkd->bhqk', qb, k, preferred_element_type=jnp.float32) * scale
        s = jnp.where(kpos[None, :] <= qpos[:, None], s, -jnp.inf)
        p = jax.nn.softmax(s, axis=-1)
        return jnp.einsum('bhqk,bhkd->bhqd', p.astype(v.dtype), v)

    return block_sweep(block, seq)


def setup_inputs(seed: int = 0) -> dict:
    key = jax.random.key(seed)
    ks = jax.random.split(key, 16)
    f32 = jnp.float32

    def nrm(k, shape, scale):
        return jax.random.normal(k, shape, f32) * scale

    def gain(k, shape):
        return 1.0 + 0.02 * jax.random.normal(k, shape, f32)

    x = jax.random.normal(ks[0], (BATCH, SEQ, D_MODEL), f32)
    positions = jnp.broadcast_to(jnp.arange(SEQ, dtype=jnp.int32)[None, :], (BATCH, SEQ))
    return {
        "x": x,
        "positions": positions,
        "norm_in": gain(ks[1], (DEPTH, D_MODEL)),
        "w_in": nrm(ks[2], (DEPTH, D_MODEL, N_IN), D_MODEL ** -0.5),
        "b_f": 3.0 + 0.1 * jax.random.normal(ks[3], (DEPTH, N_HEADS_FOX), f32),
        "q_norm_fox": gain(ks[4], (DEPTH, HEAD_DIM)),
        "k_norm_fox": gain(ks[5], (DEPTH, HEAD_DIM)),
        "cq_norm": gain(ks[6], (DEPTH, Q_LORA)),
        "w_uq": nrm(ks[7], (DEPTH, Q_LORA, N_HEADS_MLA * QK_MLA), Q_LORA ** -0.5),
        "ckv_norm": gain(ks[8], (DEPTH, KV_LORA)),
        "w_ukv": nrm(ks[9], (DEPTH, KV_LORA, N_HEADS_MLA * (QK_NOPE + V_MLA)), KV_LORA ** -0.5),
        "q_norm_mla": gain(ks[10], (DEPTH, QK_MLA)),
        "k_norm_mla": gain(ks[11], (DEPTH, QK_MLA)),
        "out_norm": gain(ks[12], (DEPTH, D_MIX)),
        "w_out": nrm(ks[13], (DEPTH, D_MIX, D_MODEL), D_MIX ** -0.5),
    }


def reference(x, positions, norm_in, w_in, b_f, q_norm_fox, k_norm_fox, cq_norm, w_uq, ckv_norm, w_ukv,
              q_norm_mla, k_norm_mla, out_norm, w_out):
    bsz, seq, _ = x.shape
    for l in range(DEPTH):
        h = rms_norm(x, norm_in[l])
        proj = jnp.einsum('bsd,dn->bsn', h, w_in[l])
        (fq, fk, fv, f_logit, sq, sk, sv, cq, ckv, k_rope_raw, gate) = jnp.split(proj, IN_SPLITS, axis=-1)

        fq = rms_norm(to_heads(fq, N_HEADS_FOX, HEAD_DIM), q_norm_fox[l])
        fk = rms_norm(to_heads(fk, N_HEADS_FOX, HEAD_DIM), k_norm_fox[l])
        fv = to_heads(fv, N_HEADS_FOX, HEAD_DIM)
        log_f = jax.nn.log_sigmoid(f_logit.astype(jnp.float32) + b_f[l].astype(jnp.float32))
        o_fox = forgetting_attention(fq.transpose(0, 2, 1, 3), fk.transpose(0, 2, 1, 3),
                                     fv.transpose(0, 2, 1, 3), log_f.transpose(0, 2, 1))

        sq = to_heads(sq, N_HEADS_SB, HEAD_DIM).transpose(0, 2, 1, 3)
        sk = to_heads(sk, N_HEADS_SB, HEAD_DIM).transpose(0, 2, 1, 3)
        sv = to_heads(sv, N_HEADS_SB, HEAD_DIM).transpose(0, 2, 1, 3)
        o_sb = stick_breaking_attention(sq, sk, sv)

        cq = rms_norm(cq, cq_norm[l])
        mq = to_heads(jnp.einsum('bsr,rn->bsn', cq, w_uq[l]), N_HEADS_MLA, QK_MLA)
        mq = rms_norm(mq, q_norm_mla[l])
        mq = jnp.concatenate([mq[..., :QK_NOPE], rope(mq[..., QK_NOPE:], positions)], axis=-1)
        ckv = rms_norm(ckv, ckv_norm[l])
        kv = to_heads(jnp.einsum('bsr,rn->bsn', ckv, w_ukv[l]), N_HEADS_MLA, QK_NOPE + V_MLA)
        k_nope, mv = kv[..., :QK_NOPE], kv[..., QK_NOPE:]
        k_rope = jnp.broadcast_to(k_rope_raw[:, :, None, :], (bsz, seq, N_HEADS_MLA, QK_ROPE))
        mk = rms_norm(jnp.concatenate([k_nope, k_rope], axis=-1), k_norm_mla[l])
        mk = jnp.concatenate([mk[..., :QK_NOPE], rope(mk[..., QK_NOPE:], positions)], axis=-1)
        o_mla = causal_softmax_attention(mq.transpose(0, 2, 1, 3), mk.transpose(0, 2, 1, 3),
                                         mv.transpose(0, 2, 1, 3))

        o = jnp.concatenate([o_fox, o_sb, o_mla], axis=2)
        o = rms_norm(o, out_norm[l].reshape(N_HEADS_TOTAL, HEAD_DIM)).reshape(bsz, seq, D_MIX)
        o = o * jax.nn.silu(gate)
        x = x + jnp.einsum('bsm,md->bsd', o, w_out[l])
    return x
```

```python
import functools

import jax
import jax.numpy as jnp
from jax import lax
from jax.experimental import pallas as pl
from jax.experimental.pallas import tpu as pltpu

HEAD_DIM = 128
N_HEADS_FOX = 12
N_HEADS_SB = 10
N_HEADS_MLA = 10
D_FOX = N_HEADS_FOX * HEAD_DIM
D_SB = N_HEADS_SB * HEAD_DIM
Q_LORA = 1536
KV_LORA = 512
QK_NOPE = 128
QK_ROPE = 64
QK_MLA = QK_NOPE + QK_ROPE
D_MLA = N_HEADS_MLA * HEAD_DIM
D_MIX = D_FOX + D_SB + D_MLA
ROPE_THETA = 10000.0
EPS = 1e-6

LANES = 128
MLA_QK_PAD = 2 * LANES
NEG_BIG = -1e30
VMEM_LIMIT = 56 * 1024 * 1024

F32 = jnp.float32
BF16 = jnp.bfloat16


def _params(*sem):
    return pltpu.CompilerParams(dimension_semantics=sem, vmem_limit_bytes=VMEM_LIMIT)


def _pick(n, pref):
    for t in pref:
        if n % t == 0:
            return t
    return n


def _rmsnorm_kernel(x_ref, g_ref, o_ref):
    x = x_ref[...]
    ms = jnp.mean(x * x, axis=-1, keepdims=True)
    o_ref[...] = (x * lax.rsqrt(ms + EPS) * g_ref[...]).astype(o_ref.dtype)


def _rmsnorm(x, g):
    s, d = x.shape
    tm = _pick(s, (256, 128, 8))
    return pl.pallas_call(
        _rmsnorm_kernel,
        out_shape=jax.ShapeDtypeStruct((s, d), BF16),
        grid=(s // tm,),
        in_specs=[pl.BlockSpec((tm, d), lambda i: (i, 0)), pl.BlockSpec((1, d), lambda i: (0, 0))],
        out_specs=pl.BlockSpec((tm, d), lambda i: (i, 0)),
        compiler_params=_params("parallel"),
        name="rmsnorm",
    )(x, g.reshape(1, d))


def _ep_headnorm(acc, g_ref, o_ref):
    for c in range(acc.shape[1] // LANES):
        a = acc[:, c * LANES:(c + 1) * LANES]
        ms = jnp.mean(a * a, axis=-1, keepdims=True)
        o_ref[:, c * LANES:(c + 1) * LANES] = (
            a * lax.rsqrt(ms + EPS) * g_ref[:, c * LANES:(c + 1) * LANES]).astype(o_ref.dtype)


def _ep_colscale(acc, s_ref, o_ref):
    o_ref[...] = (acc * s_ref[...]).astype(o_ref.dtype)


def _ep_silu(acc, o_ref):
    o_ref[...] = (acc * jax.nn.sigmoid(acc)).astype(o_ref.dtype)


def _ep_rownorm(acc, g_ref, o_ref):
    ms = jnp.mean(acc * acc, axis=-1, keepdims=True)
    o_ref[...] = (acc * lax.rsqrt(ms + EPS) * g_ref[...]).astype(o_ref.dtype)


def _ep_ckv_small(acc, g_ref, ckv_ref, small_ref):
    a = acc[:, :KV_LORA]
    ms = jnp.mean(a * a, axis=-1, keepdims=True)
    ckv_ref[...] = (a * lax.rsqrt(ms + EPS) * g_ref[...]).astype(ckv_ref.dtype)
    small_ref[...] = acc[:, KV_LORA:]


def _mm_kernel(h_ref, w_ref, *rest, epilogue):
    acc = jnp.dot(h_ref[...], w_ref[...], preferred_element_type=F32)
    epilogue(acc, *rest)


def _project(h, w, aux, epilogue, outs, tm, tn):
    m, k = h.shape
    n = w.shape[1]
    grid = (n // tn, m // tm)
    in_specs = [pl.BlockSpec((tm, k), lambda j, i: (i, 0)), pl.BlockSpec((k, tn), lambda j, i: (0, j))]
    for a in aux:
        if a.shape[1] == n:
            in_specs.append(pl.BlockSpec((1, tn), lambda j, i: (0, j)))
        else:
            in_specs.append(pl.BlockSpec(a.shape, lambda j, i: (0, 0)))
    out_shape, out_specs = [], []
    for cols, dt in outs:
        out_shape.append(jax.ShapeDtypeStruct((m, cols), dt))
        if cols == n:
            out_specs.append(pl.BlockSpec((tm, tn), lambda j, i: (i, j)))
        else:
            out_specs.append(pl.BlockSpec((tm, cols), lambda j, i: (i, 0)))
    res = pl.pallas_call(
        functools.partial(_mm_kernel, epilogue=epilogue),
        out_shape=out_shape,
        grid=grid,
        in_specs=in_specs,
        out_specs=out_specs,
        compiler_params=_params("parallel", "parallel"),
        name="proj_" + epilogue.__name__.removeprefix("_ep_"),
    )(h, w, *aux)
    return res


def _forget_kernel(x_ref, b_ref, tri_ref, cq_ref, ck_ref, carry_ref):
    @pl.when(pl.program_id(0) == 0)
    def _():
        carry_ref[...] = jnp.zeros_like(carry_ref)

    x = x_ref[...] + b_ref[...]
    lf = jnp.minimum(x, 0.0) - jnp.log1p(jnp.exp(-jnp.abs(x)))
    hi = lf.astype(BF16)
    r1 = lf - hi.astype(F32)
    mid = r1.astype(BF16)
    lo = (r1 - mid.astype(F32)).astype(BF16)
    tri = tri_ref[...]
    c = (jnp.dot(tri, hi, preferred_element_type=F32) + jnp.dot(tri, mid, preferred_element_type=F32)
         + jnp.dot(tri, lo, preferred_element_type=F32)) + carry_ref[...]
    tc = c.shape[0]
    carry_ref[...] = c[tc - 1:tc, :]
    ck_ref[...] = c.T
    for h in range(N_HEADS_FOX):
        cq_ref[h] = jnp.broadcast_to(c[:, h:h + 1], (tc, LANES))


def _forget_cumsum(small, b_pad):
    s = small.shape[0]
    tc = _pick(s, (256, 128))
    tri = jnp.tril(jnp.ones((tc, tc), F32)).astype(BF16)
    cq, ck = pl.pallas_call(
        _forget_kernel,
        out_shape=[jax.ShapeDtypeStruct((N_HEADS_FOX, s, LANES), F32), jax.ShapeDtypeStruct((LANES, s), F32)],
        grid=(s // tc,),
        in_specs=[pl.BlockSpec((tc, LANES), lambda i: (i, 0)), pl.BlockSpec((1, LANES), lambda i: (0, 0)),
                  pl.BlockSpec((tc, tc), lambda i: (0, 0))],
        out_specs=[pl.BlockSpec((N_HEADS_FOX, tc, LANES), lambda i: (0, i, 0)),
                   pl.BlockSpec((LANES, tc), lambda i: (0, i))],
        scratch_shapes=[pltpu.VMEM((1, LANES), F32)],
        compiler_params=_params("arbitrary"),
        name="forget_cumsum",
    )(small, b_pad, tri)
    return cq, ck


def _rope_pad(r, cos, sin_signed):
    return r * cos + pltpu.roll(r, 64, 1) * sin_signed


def _mla_prep_kernel(cq_ref, ckv_ref, small_ref, pos_ref, inv_ref, sgn_ref, wuq_ref, wukv_ref, gq_ref, gk_ref,
                     mq_ref, mk_ref, mv_ref, *, q_scale):
    cqn = cq_ref[...]
    ckvn = ckv_ref[...]
    kr = small_ref[...]
    ang = pos_ref[...].astype(F32) * inv_ref[...]
    cos = jnp.cos(ang)
    sin_signed = jnp.sin(ang) * sgn_ref[...]
    gq = gq_ref[...]
    gk = gk_ref[...]
    kr_ss = jnp.sum(kr * kr, axis=-1, keepdims=True)
    for h in range(N_HEADS_MLA):
        qh = jnp.dot(cqn, wuq_ref[h], preferred_element_type=F32)
        ms = jnp.sum(qh * qh, axis=-1, keepdims=True) * (1.0 / QK_MLA)
        y = qh * lax.rsqrt(ms + EPS) * gq
        mq_ref[:, h * MLA_QK_PAD:h * MLA_QK_PAD + LANES] = (y[:, :LANES] * q_scale).astype(mq_ref.dtype)
        mq_ref[:, h * MLA_QK_PAD + LANES:(h + 1) * MLA_QK_PAD] = (
            _rope_pad(y[:, LANES:], cos, sin_signed) * q_scale).astype(mq_ref.dtype)
        kvh = jnp.dot(ckvn, wukv_ref[h], preferred_element_type=F32)
        kn = kvh[:, :LANES]
        ms = (jnp.sum(kn * kn, axis=-1, keepdims=True) + kr_ss) * (1.0 / QK_MLA)
        rs = lax.rsqrt(ms + EPS)
        mk_ref[:, h * MLA_QK_PAD:h * MLA_QK_PAD + LANES] = (kn * rs * gk[:, :LANES]).astype(mk_ref.dtype)
        mk_ref[:, h * MLA_QK_PAD + LANES:(h + 1) * MLA_QK_PAD] = _rope_pad(
            kr * rs * gk[:, LANES:], cos, sin_signed).astype(mk_ref.dtype)
        mv_ref[:, h * LANES:(h + 1) * LANES] = kvh[:, LANES:].astype(mv_ref.dtype)


def _pad_rope_lanes(a):
    z = jnp.zeros(a.shape[:-1] + (32,), a.dtype)
    return jnp.concatenate([a[..., :32], z, a[..., 32:], z], axis=-1)


def _mla_prep(cqn, ckvn, small, positions, w_uq, w_ukv, g_q, g_k):
    s = cqn.shape[0]
    tm = _pick(s, (256, 128, 8))
    half = QK_ROPE // 2
    inv = ROPE_THETA ** (-jnp.arange(half, dtype=F32) / half)
    inv_pad = _pad_rope_lanes(jnp.concatenate([inv, inv]))[None, :]
    sgn_pad = _pad_rope_lanes(jnp.concatenate([-jnp.ones((half,), F32), jnp.ones((half,), F32)]))[None, :]
    wq = w_uq.reshape(Q_LORA, N_HEADS_MLA, QK_MLA)
    wq = jnp.concatenate([wq[..., :QK_NOPE], _pad_rope_lanes(wq[..., QK_NOPE:])], axis=-1)
    wq = wq.transpose(1, 0, 2).astype(BF16)
    wkv = w_ukv.reshape(KV_LORA, N_HEADS_MLA, QK_NOPE + HEAD_DIM).transpose(1, 0, 2).astype(BF16)
    gq = jnp.concatenate([g_q[:QK_NOPE], _pad_rope_lanes(g_q[QK_NOPE:])])[None, :]
    gk = jnp.concatenate([g_k[:QK_NOPE], _pad_rope_lanes(g_k[QK_NOPE:])])[None, :]
    const2 = lambda i: (0, 0)
    const3 = lambda i: (0, 0, 0)
    return pl.pallas_call(
        functools.partial(_mla_prep_kernel, q_scale=QK_MLA ** -0.5),
        out_shape=[jax.ShapeDtypeStruct((s, N_HEADS_MLA * MLA_QK_PAD), BF16),
                   jax.ShapeDtypeStruct((s, N_HEADS_MLA * MLA_QK_PAD), BF16),
                   jax.ShapeDtypeStruct((s, D_MLA), BF16)],
        grid=(s // tm,),
        in_specs=[pl.BlockSpec((tm, Q_LORA), lambda i: (i, 0)),
                  pl.BlockSpec((tm, KV_LORA), lambda i: (i, 0)),
                  pl.BlockSpec((tm, LANES), lambda i: (i, 1)),
                  pl.BlockSpec((tm, 1), lambda i: (i, 0)),
                  pl.BlockSpec((1, LANES), const2),
                  pl.BlockSpec((1, LANES), const2),
                  pl.BlockSpec((N_HEADS_MLA, Q_LORA, MLA_QK_PAD), const3),
                  pl.BlockSpec((N_HEADS_MLA, KV_LORA, QK_NOPE + HEAD_DIM), const3),
                  pl.BlockSpec((1, MLA_QK_PAD), const2),
                  pl.BlockSpec((1, MLA_QK_PAD), const2)],
        out_specs=[pl.BlockSpec((tm, N_HEADS_MLA * MLA_QK_PAD), lambda i: (i, 0)),
                   pl.BlockSpec((tm, N_HEADS_MLA * MLA_QK_PAD), lambda i: (i, 0)),
                   pl.BlockSpec((tm, D_MLA), lambda i: (i, 0))],
        compiler_params=_params("parallel"),
        name="mla_prep",
    )(cqn, ckvn, small, positions.reshape(s, 1), inv_pad, sgn_pad, wq, wkv, gq, gk)


def _lane_tile(a, width):
    return jnp.concatenate([a] * (width // LANES), axis=1)


def _qk(q, kj):
    return lax.dot_general(q, kj, (((1,), (1,)), ((), ())), preferred_element_type=F32)


def _finish_head(o, gate_ref, gn_ref, o_ref):
    ms = jnp.mean(o * o, axis=-1, keepdims=True)
    o_ref[...] = (o * lax.rsqrt(ms + EPS) * gn_ref[...] * gate_ref[...]).astype(o_ref.dtype)


def _softmax_attn_kernel(*refs, tq, tk, has_bias):
    if has_bias:
        q_ref, k_ref, v_ref, cq_ref, ck_ref, gate_ref, gn_ref, o_ref, m_sc, l_sc, acc_sc = refs
    else:
        q_ref, k_ref, v_ref, gate_ref, gn_ref, o_ref, m_sc, l_sc, acc_sc = refs
    i = pl.program_id(1)
    n_diag = tq // tk
    q = q_ref[...]
    m_sc[...] = jnp.full_like(m_sc, NEG_BIG)
    l_sc[...] = jnp.zeros_like(l_sc)
    acc_sc[...] = jnp.zeros_like(acc_sc)
    if has_bias:
        cq = _lane_tile(cq_ref[0], tk)

    def step(j, masked):
        start = pl.multiple_of(j * tk, tk)
        kj = k_ref[pl.ds(start, tk), :]
        vj = v_ref[pl.ds(start, tk), :]
        s = _qk(q, kj)
        if has_bias:
            s = s + (cq - ck_ref[0, j])
        if masked:
            row = i * tq + lax.broadcasted_iota(jnp.int32, (tq, tk), 0)
            col = j * tk + lax.broadcasted_iota(jnp.int32, (tq, tk), 1)
            s = jnp.where(col <= row, s, NEG_BIG)
        m_prev = m_sc[...]
        m_new = jnp.maximum(m_prev, jnp.max(s, axis=-1, keepdims=True))
        alpha = jnp.exp(m_prev - m_new)
        p = jnp.exp(s - _lane_tile(m_new, tk))
        l_sc[...] = alpha * l_sc[...] + jnp.sum(p, axis=-1, keepdims=True)
        acc_sc[...] = alpha * acc_sc[...] + jnp.dot(p.astype(BF16), vj, preferred_element_type=F32)
        m_sc[...] = m_new

    def body(j, c):
        step(j, False)
        return c

    lax.fori_loop(0, i * n_diag, body, 0)
    for d in range(n_diag):
        step(i * n_diag + d, True)
    _finish_head(acc_sc[...] / l_sc[...], gate_ref, gn_ref, o_ref)


def _sb_attn_kernel(q_ref, k_ref, v_ref, u_ref, gate_ref, gn_ref, o_ref, r_sc, acc_sc, *, tq, tk):
    i = pl.program_id(1)
    n_diag = tq // tk
    q = q_ref[...]
    r_sc[...] = jnp.zeros_like(r_sc)
    acc_sc[...] = jnp.zeros_like(acc_sc)

    def step(j, masked):
        start = pl.multiple_of(j * tk, tk)
        kj = k_ref[pl.ds(start, tk), :]
        vj = v_ref[pl.ds(start, tk), :]
        z = _qk(q, kj)
        ls = jnp.minimum(z, 0.0) - jnp.log1p(jnp.exp(-jnp.abs(z)))
        l1m = ls - z
        if masked:
            row = i * tq + lax.broadcasted_iota(jnp.int32, (tq, tk), 0)
            col = j * tk + lax.broadcasted_iota(jnp.int32, (tq, tk), 1)
            causal = col < row
            l1m = jnp.where(causal, l1m, 0.0)
        hi = l1m.astype(BF16)
        lo = (l1m - hi.astype(F32)).astype(BF16)
        u = u_ref[...]
        tail = jnp.dot(hi, u, preferred_element_type=F32) + jnp.dot(lo, u, preferred_element_type=F32)
        r = r_sc[...]
        log_a = ls + tail + _lane_tile(r, tk)
        if masked:
            log_a = jnp.where(causal, log_a, NEG_BIG)
        a = jnp.exp(log_a)
        acc_sc[...] += jnp.dot(a.astype(BF16), vj, preferred_element_type=F32)
        r_sc[...] = r + (tail[:, :1] + l1m[:, :1])

    for d in reversed(range(n_diag)):
        step(i * n_diag + d, True)

    def body(jj, c):
        step(i * n_diag - 1 - jj, False)
        return c

    lax.fori_loop(0, i * n_diag, body, 0)
    _finish_head(acc_sc[...], gate_ref, gn_ref, o_ref)


def _attention(kind, q, k, v, dq, gate, gn, gate_off, n_heads, tq, tk, bias=None):
    (q, q0), (k, k0), (v, v0) = q, k, v
    s = q.shape[0]
    goff = gate_off // LANES
    q0, k0, v0 = q0 // dq, k0 // dq, v0 // LANES
    in_specs = [pl.BlockSpec((tq, dq), lambda h, i: (i, q0 + h)),
                pl.BlockSpec((s, dq), lambda h, i: (0, k0 + h)),
                pl.BlockSpec((s, LANES), lambda h, i: (0, v0 + h))]
    args = [q, k, v]
    scratch = [pltpu.VMEM((tq, LANES), F32), pltpu.VMEM((tq, LANES), F32)]
    if kind == "sb":
        u = (jnp.arange(tk)[:, None] > jnp.arange(tk)[None, :]).astype(BF16)
        in_specs.append(pl.BlockSpec((tk, tk), lambda h, i: (0, 0)))
        args.append(u)
        body = functools.partial(_sb_attn_kernel, tq=tq, tk=tk)
    else:
        if bias is not None:
            cq, ck = bias
            ck = ck.reshape(ck.shape[0], s // tk, 1, tk)
            in_specs += [pl.BlockSpec((1, tq, LANES), lambda h, i: (h, i, 0)),
                         pl.BlockSpec((1, s // tk, 1, tk), lambda h, i: (h, 0, 0, 0))]
            args += [cq, ck]
        scratch.append(pltpu.VMEM((tq, LANES), F32))
        body = functools.partial(_softmax_attn_kernel, tq=tq, tk=tk, has_bias=bias is not None)
    in_specs += [pl.BlockSpec((tq, LANES), lambda h, i: (i, goff + h)),
                 pl.BlockSpec((1, LANES), lambda h, i: (0, goff + h))]
    args += [gate, gn]
    return pl.pallas_call(
        body,
        out_shape=jax.ShapeDtypeStruct((s, n_heads * LANES), BF16),
        grid=(n_heads, s // tq),
        in_specs=in_specs,
        out_specs=pl.BlockSpec((tq, LANES), lambda h, i: (i, h)),
        scratch_shapes=scratch,
        compiler_params=_params("parallel", "arbitrary"),
        name="attn_" + kind,
    )(*args)


def _out_kernel(x_ref, of_ref, os_ref, om_ref, wf_ref, ws_ref, wm_ref, o_ref):
    acc = jnp.dot(of_ref[...], wf_ref[...], preferred_element_type=F32)
    acc += jnp.dot(os_ref[...], ws_ref[...], preferred_element_type=F32)
    acc += jnp.dot(om_ref[...], wm_ref[...], preferred_element_type=F32)
    o_ref[...] = x_ref[...] + acc


def _out_project(x, o_f, o_s, o_m, w_f, w_s, w_m):
    s, d = x.shape
    tm = _pick(s, (512, 256, 128, 8))
    tn = _pick(d, (1024, 512, 256, 128))
    return pl.pallas_call(
        _out_kernel,
        out_shape=jax.ShapeDtypeStruct((s, d), F32),
        grid=(d // tn, s // tm),
        in_specs=[pl.BlockSpec((tm, tn), lambda j, i: (i, j)),
                  pl.BlockSpec((tm, D_FOX), lambda j, i: (i, 0)),
                  pl.BlockSpec((tm, D_SB), lambda j, i: (i, 0)),
                  pl.BlockSpec((tm, D_MLA), lambda j, i: (i, 0)),
                  pl.BlockSpec((D_FOX, tn), lambda j, i: (0, j)),
                  pl.BlockSpec((D_SB, tn), lambda j, i: (0, j)),
                  pl.BlockSpec((D_MLA, tn), lambda j, i: (0, j))],
        out_specs=pl.BlockSpec((tm, tn), lambda j, i: (i, j)),
        compiler_params=_params("parallel", "parallel"),
        name="out_proj",
    )(x, o_f, o_s, o_m, w_f, w_s, w_m)


def _layer(x, positions, norm_in, w_in, b_f, q_norm_fox, k_norm_fox, cq_norm, w_uq, ckv_norm, w_ukv,
           q_norm_mla, k_norm_mla, out_norm, w_out):
    s, d = x.shape
    tm = _pick(s, (512, 256, 128, 8))
    tq = _pick(s, (512, 256, 128))
    tk_sb = 256 if tq % 256 == 0 else tq

    o = 0
    seg = {}
    for name, width in (("fq", D_FOX), ("fk", D_FOX), ("fv", D_FOX), ("fl", N_HEADS_FOX), ("sq", D_SB),
                        ("sk", D_SB), ("sv", D_SB), ("cq", Q_LORA), ("ckv", KV_LORA), ("kr", QK_ROPE),
                        ("gate", D_MIX)):
        seg[name] = (o, o + width)
        o += width
    col = lambda name: w_in[:, seg[name][0]:seg[name][1]]

    h = _rmsnorm(x, norm_in)

    w_qk = jnp.concatenate([col("fq"), col("fk")], axis=1).astype(BF16)
    g_qk = jnp.concatenate([jnp.tile(q_norm_fox * HEAD_DIM ** -0.5, N_HEADS_FOX),
                            jnp.tile(k_norm_fox, N_HEADS_FOX)])[None, :]
    (fqk,) = _project(h, w_qk, [g_qk], _ep_headnorm, [(2 * D_FOX, BF16)], tm, _pick(2 * D_FOX, (1024, 768, 512, 384)))

    w_pl = jnp.concatenate([col("fv"), col("sq"), col("sk"), col("sv")], axis=1).astype(BF16)
    n_pl = D_FOX + 3 * D_SB
    s_pl = jnp.concatenate([jnp.ones((D_FOX,), F32), jnp.full((D_SB,), HEAD_DIM ** -0.5, F32),
                            jnp.ones((2 * D_SB,), F32)])[None, :]
    (plain,) = _project(h, w_pl, [s_pl], _ep_colscale, [(n_pl, BF16)], tm, _pick(n_pl, (768, 512, 384, 256, 128)))

    (gate,) = _project(h, col("gate").astype(BF16), [], _ep_silu, [(D_MIX, F32)], tm, _pick(D_MIX, (1024, 512, 256)))

    (cqn,) = _project(h, col("cq").astype(BF16), [cq_norm[None, :]], _ep_rownorm, [(Q_LORA, BF16)], tm, Q_LORA)

    w_small = jnp.concatenate([col("ckv"), col("fl"), jnp.zeros((d, LANES - N_HEADS_FOX), F32),
                               _pad_rope_lanes(col("kr"))], axis=1).astype(BF16)
    ckvn, small = _project(h, w_small, [ckv_norm[None, :]], _ep_ckv_small,
                           [(KV_LORA, BF16), (2 * LANES, F32)], tm, KV_LORA + 2 * LANES)

    b_pad = jnp.concatenate([b_f, jnp.zeros((LANES - N_HEADS_FOX,), F32)])[None, :]
    cum_q, cum_k = _forget_cumsum(small, b_pad)

    mq, mk, mv = _mla_prep(cqn, ckvn, small, positions, w_uq, w_ukv, q_norm_mla, k_norm_mla)

    gn = out_norm[None, :]
    o_f = _attention("fox", (fqk, 0), (fqk, D_FOX), (plain, 0), HEAD_DIM, gate, gn, 0, N_HEADS_FOX, tq, tq,
                     bias=(cum_q, cum_k))
    o_s = _attention("sb", (plain, D_FOX), (plain, D_FOX + D_SB), (plain, D_FOX + 2 * D_SB), HEAD_DIM, gate, gn,
                     D_FOX, N_HEADS_SB, tq, tk_sb)
    o_m = _attention("mla", (mq, 0), (mk, 0), (mv, 0), MLA_QK_PAD, gate, gn, D_FOX + D_SB, N_HEADS_MLA, tq, tq)

    w_o = w_out.astype(BF16)
    return _out_project(x, o_f, o_s, o_m, w_o[:D_FOX], w_o[D_FOX:D_FOX + D_SB], w_o[D_FOX + D_SB:])


def kernel(x, positions, norm_in, w_in, b_f, q_norm_fox, k_norm_fox, cq_norm, w_uq, ckv_norm, w_ukv, q_norm_mla,
           k_norm_mla, out_norm, w_out):
    bsz, seq, d = x.shape
    outs = []
    for b in range(bsz):
        xb = x[b]
        for l in range(norm_in.shape[0]):
            xb = _layer(xb, positions[b], norm_in[l], w_in[l], b_f[l], q_norm_fox[l], k_norm_fox[l], cq_norm[l],
                        w_uq[l], ckv_norm[l], w_ukv[l], q_norm_mla[l], k_norm_mla[l], out_norm[l], w_out[l])
        outs.append(xb)
    return jnp.stack(outs, axis=0)
```
